```python
import math
import jax, jax.numpy as jnp
from jax import lax
import numpy as np

D_MODEL = 2048
BATCH = 2
SEQ = 4096
DEPTH = 4

HEAD_DIM = 128
BLOCK = 128
GMLP_WIDTH = D_MODEL // 2
GMLP_GROUPS = 4
GMLP_GROUP_CH = GMLP_WIDTH // GMLP_GROUPS
FOX_HEADS = D_MODEL // 256
FOX_WIDTH = FOX_HEADS * HEAD_DIM
DIFF_HEADS = D_MODEL // 512
DIFF_QK_WIDTH = DIFF_HEADS * 2 * HEAD_DIM
DIFF_V_DIM = 2 * HEAD_DIM
DIFF_WIDTH = DIFF_HEADS * DIFF_V_DIM
N_BRANCH = 3
D_FF = 4 * D_MODEL
ROPE_THETA = 10000.0
LN_EPS = 1e-5
RMS_EPS = 1e-5
NEG_INF = -1e30
DEEPNORM_ALPHA = (2 * DEPTH) ** 0.25
DEEPNORM_BETA = (8 * DEPTH) ** -0.25

OFF_U = 0
OFF_V = OFF_U + GMLP_WIDTH
OFF_FQ = OFF_V + GMLP_WIDTH
OFF_FK = OFF_FQ + FOX_WIDTH
OFF_FV = OFF_FK + FOX_WIDTH
OFF_FF = OFF_FV + FOX_WIDTH
OFF_DQ = OFF_FF + FOX_HEADS
OFF_DK = OFF_DQ + DIFF_QK_WIDTH
OFF_DV = OFF_DK + DIFF_QK_WIDTH
IN_WIDTH = OFF_DV + DIFF_WIDTH

kernel_name = "hybrid_gated_gmlp_fox_diffattn_deepnorm"


def layer_norm(x, g, b):
    xf = x.astype(jnp.float32)
    mu = jnp.mean(xf, axis=-1, keepdims=True)
    var = jnp.mean(jnp.square(xf - mu), axis=-1, keepdims=True)
    y = (xf - mu) * lax.rsqrt(var + LN_EPS)
    return (y * g.astype(jnp.float32) + b.astype(jnp.float32)).astype(x.dtype)


def rms_norm(x, g):
    xf = x.astype(jnp.float32)
    y = xf * lax.rsqrt(jnp.mean(jnp.square(xf), axis=-1, keepdims=True) + RMS_EPS)
    return y * g.astype(jnp.float32)


def apply_rope(x, cos, sin):
    half = x.shape[-1] // 2
    x1, x2 = x[..., :half], x[..., half:]
    return jnp.concatenate([x1 * cos - x2 * sin, x2 * cos + x1 * sin], axis=-1)


def gmlp_spatial_gating(u, v, ln_g, ln_b, w_s, b_s):
    B, S, _ = v.shape
    v = layer_norm(v, ln_g, ln_b)
    vb = v.reshape(B, S // BLOCK, BLOCK, GMLP_GROUPS, GMLP_GROUP_CH)
    causal = jnp.tril(jnp.ones((BLOCK, BLOCK), dtype=w_s.dtype))
    w = w_s * causal[None]
    sv = jnp.einsum('gts,bcsgk->bctgk', w, vb) + jnp.transpose(b_s)[None, None, :, :, None]
    return u * sv.reshape(B, S, GMLP_WIDTH)


def block_causal_attention(q, k, v, map_coef, decay_cum=None):
    B, H, M, S, Dk = q.shape
    Dv = v.shape[-1]
    nb = S // BLOCK
    scale = Dk ** -0.5
    key_pos = jnp.arange(S)

    def one_block(i):
        start = i * BLOCK
        qi = lax.dynamic_slice_in_dim(q, start, BLOCK, axis=3)
        s = jnp.einsum('bhmqd,bhmkd->bhmqk', qi, k,
                       preferred_element_type=jnp.float32) * scale
        if decay_cum is not None:
            cq = lax.dynamic_slice_in_dim(decay_cum, start, BLOCK, axis=2)
            s = s + (cq[:, :, :, None] - decay_cum[:, :, None, :])[:, :, None]
        q_pos = start + jnp.arange(BLOCK)
        mask = key_pos[None, :] <= q_pos[:, None]
        s = jnp.where(mask, s, NEG_INF)
        p = jax.nn.softmax(s, axis=-1)
        p = jnp.einsum('m,bhmqk->bhqk', map_coef, p)
        return jnp.einsum('bhqk,bhkd->bhqd', p.astype(v.dtype), v)

    out = lax.map(one_block, jnp.arange(nb))
    return jnp.moveaxis(out, 0, 2).reshape(B, H, S, Dv)


def setup_inputs(seed: int = 0) -> dict:
    key = jax.random.key(seed)
    ks = jax.random.split(key, 26)
    f32 = jnp.float32

    def nrm(k, shape, scale):
        return jax.random.normal(k, shape, f32) * scale

    L, D = DEPTH, D_MODEL
    return {
        "x": nrm(ks[0], (BATCH, SEQ, D), 1.0),
        "w_in": nrm(ks[1], (L, D, IN_WIDTH), D ** -0.5),
        "b_forget": 3.0 + nrm(ks[2], (L, FOX_HEADS), 0.5),
        "gmlp_ln_g": 1.0 + nrm(ks[3], (L, GMLP_WIDTH), 0.02),
        "gmlp_ln_b": nrm(ks[4], (L, GMLP_WIDTH), 0.02),
        "gmlp_w_s": nrm(ks[5], (L, GMLP_GROUPS, BLOCK, BLOCK), 0.5 * BLOCK ** -0.5),
        "gmlp_b_s": 1.0 + nrm(ks[6], (L, GMLP_GROUPS, BLOCK), 0.02),
        "lam_q1": nrm(ks[7], (L, HEAD_DIM), 0.1),
        "lam_k1": nrm(ks[8], (L, HEAD_DIM), 0.1),
        "lam_q2": nrm(ks[9], (L, HEAD_DIM), 0.1),
        "lam_k2": nrm(ks[10], (L, HEAD_DIM), 0.1),
        "diff_norm_g": 1.0 + nrm(ks[11], (L, DIFF_V_DIM), 0.02),
        "w_branch_a": nrm(ks[12], (L, GMLP_WIDTH, D), GMLP_WIDTH ** -0.5),
        "w_branch_b": nrm(ks[13], (L, FOX_WIDTH, D), FOX_WIDTH ** -0.5),
        "w_branch_c": nrm(ks[14], (L, DIFF_WIDTH, D), DIFF_WIDTH ** -0.5),
        "w_gate": nrm(ks[15], (L, D, N_BRANCH * D), D ** -0.5),
        "b_gate": nrm(ks[16], (L, N_BRANCH * D), 0.02),
        "w_out": nrm(ks[17], (L, D, D), DEEPNORM_BETA * D ** -0.5),
        "ln_mix_g": 1.0 + nrm(ks[18], (L, D), 0.02),
        "ln_mix_b": nrm(ks[19], (L, D), 0.02),
        "w_up": nrm(ks[20], (L, D, D_FF), D ** -0.5),
        "w_down": nrm(ks[21], (L, D_FF, D), DEEPNORM_BETA * D_FF ** -0.5),
        "ln_mlp_g": 1.0 + nrm(ks[22], (L, D), 0.02),
        "ln_mlp_b": nrm(ks[23], (L, D), 0.02),
    }


def reference(x, w_in, b_forget, gmlp_ln_g, gmlp_ln_b, gmlp_w_s, gmlp_b_s,
              lam_q1, lam_k1, lam_q2, lam_k2, diff_norm_g,
              w_branch_a, w_branch_b, w_branch_c, w_gate, b_gate, w_out,
              ln_mix_g, ln_mix_b, w_up, w_down, ln_mlp_g, ln_mlp_b):
    B, S, D = x.shape
    dt = x.dtype

    pos = jnp.arange(S, dtype=jnp.float32)
    inv_freq = ROPE_THETA ** (-jnp.arange(0, HEAD_DIM, 2, dtype=jnp.float32) / HEAD_DIM)
    ang = pos[:, None] * inv_freq[None, :]
    cos = jnp.cos(ang).astype(dt)[None, :, None, None, :]
    sin = jnp.sin(ang).astype(dt)[None, :, None, None, :]
    fox_coef = jnp.ones((1,), jnp.float32)

    for l in range(DEPTH):
        h = x
        proj = h @ w_in[l]

        u = jax.nn.gelu(proj[..., OFF_U:OFF_V], approximate=False)
        v = jax.nn.gelu(proj[..., OFF_V:OFF_FQ], approximate=False)
        out_a = gmlp_spatial_gating(u, v, gmlp_ln_g[l], gmlp_ln_b[l], gmlp_w_s[l], gmlp_b_s[l])

        fq = proj[..., OFF_FQ:OFF_FK].reshape(B, S, FOX_HEADS, HEAD_DIM).transpose(0, 2, 1, 3)[:, :, None]
        fk = proj[..., OFF_FK:OFF_FV].reshape(B, S, FOX_HEADS, HEAD_DIM).transpose(0, 2, 1, 3)[:, :, None]
        fv = proj[..., OFF_FV:OFF_FF].reshape(B, S, FOX_HEADS, HEAD_DIM).transpose(0, 2, 1, 3)
        log_f = jax.nn.log_sigmoid((proj[..., OFF_FF:OFF_DQ] + b_forget[l]).astype(jnp.float32))
        c = jnp.cumsum(log_f, axis=1).transpose(0, 2, 1)
        fo = block_causal_attention(fq, fk, fv, fox_coef, c)
        out_b = fo.transpose(0, 2, 1, 3).reshape(B, S, FOX_WIDTH)

        dq = apply_rope(proj[..., OFF_DQ:OFF_DK].reshape(B, S, DIFF_HEADS, 2, HEAD_DIM), cos, sin)
        dk = apply_rope(proj[..., OFF_DK:OFF_DV].reshape(B, S, DIFF_HEADS, 2, HEAD_DIM), cos, sin)
        dq = dq.transpose(0, 2, 3, 1, 4)
        dk = dk.transpose(0, 2, 3, 1, 4)
        dv = proj[..., OFF_DV:IN_WIDTH].reshape(B, S, DIFF_HEADS, DIFF_V_DIM).transpose(0, 2, 1, 3)
        lam_init = 0.8 - 0.6 * math.exp(-0.3 * l)
        lam = (jnp.exp(jnp.sum(lam_q1[l].astype(jnp.float32) * lam_k1[l].astype(jnp.float32)))
               - jnp.exp(jnp.sum(lam_q2[l].astype(jnp.float32) * lam_k2[l].astype(jnp.float32)))
               + lam_init)
        diff_coef = jnp.stack([jnp.ones((), jnp.float32), -lam])
        do = block_causal_attention(dq, dk, dv, diff_coef)
        do = rms_norm(do, diff_norm_g[l]) * (1.0 - lam_init)
        out_c = do.astype(dt).transpose(0, 2, 1, 3).reshape(B, S, DIFF_WIDTH)

        gates = jax.nn.sigmoid(h @ w_gate[l] + b_gate[l]).reshape(B, S, N_BRANCH, D)
        merged = (gates[:, :, 0] * (out_a @ w_branch_a[l])
                  + gates[:, :, 1] * (out_b @ w_branch_b[l])
                  + gates[:, :, 2] * (out_c @ w_branch_c[l]))
        mix = merged @ w_out[l]
        x = layer_norm(DEEPNORM_ALPHA * x + mix, ln_mix_g[l], ln_mix_b[l])

        ff = jnp.square(jax.nn.relu(x @ w_up[l])) @ w_down[l]
        x = layer_norm(DEEPNORM_ALPHA * x + ff, ln_mlp_g[l], ln_mlp_b[l])

    return x
```

```python
import functools
import math

import jax
import jax.numpy as jnp
from jax import lax
from jax.experimental import pallas as pl
from jax.experimental.pallas import tpu as pltpu

D_MODEL = 2048
DEPTH = 4
HEAD_DIM = 128
BLOCK = 128
GMLP_WIDTH = D_MODEL // 2
GMLP_GROUPS = 4
GMLP_GROUP_CH = GMLP_WIDTH // GMLP_GROUPS
FOX_HEADS = D_MODEL // 256
FOX_WIDTH = FOX_HEADS * HEAD_DIM
DIFF_HEADS = D_MODEL // 512
DIFF_QK_WIDTH = DIFF_HEADS * 2 * HEAD_DIM
DIFF_V_DIM = 2 * HEAD_DIM
DIFF_WIDTH = DIFF_HEADS * DIFF_V_DIM
N_BRANCH = 3
D_FF = 4 * D_MODEL
ROPE_THETA = 10000.0
LN_EPS = 1e-5
RMS_EPS = 1e-5
NEG_INF = -1e30
DEEPNORM_ALPHA = (2 * DEPTH) ** 0.25

OFF_V = GMLP_WIDTH
OFF_FQ = OFF_V + GMLP_WIDTH
OFF_FF = OFF_FQ + 3 * FOX_WIDTH
OFF_DQ = OFF_FF + FOX_HEADS
IN_WIDTH = OFF_DQ + 2 * DIFF_QK_WIDTH + DIFF_WIDTH

ATT_WIDTH = 3 * FOX_WIDTH + 2 * DIFF_QK_WIDTH + DIFF_WIDTH
QK_SCALE = HEAD_DIM ** -0.5
LANES = 128
VMEM_LIMIT = 56 * 1024 * 1024

BF16 = jnp.bfloat16
F32 = jnp.float32


def _params(*sem):
    return pltpu.CompilerParams(dimension_semantics=sem, vmem_limit_bytes=VMEM_LIMIT)


def _dot(a, b):
    return jnp.dot(a, b, preferred_element_type=F32)


def _dot_nt(a, b):
    return lax.dot_general(a, b, (((1,), (1,)), ((), ())), preferred_element_type=F32)


def _gelu(x):
    return 0.5 * x * (1.0 + lax.erf(x * (2.0 ** -0.5)))


def _layer_norm(y, g, b):
    mu = jnp.mean(y, axis=-1, keepdims=True)
    d = y - mu
    var = jnp.mean(d * d, axis=-1, keepdims=True)
    return d * lax.rsqrt(var + LN_EPS) * g + b


def _gmlp_kernel(x_ref, w_ref, g_ref, b_ref, ws_ref, bs_ref, o_ref):
    x = x_ref[...]
    u = _gelu(_dot(x, w_ref[:, :GMLP_WIDTH]))
    v = _gelu(_dot(x, w_ref[:, GMLP_WIDTH:]))
    vn = _layer_norm(v, g_ref[...], b_ref[...]).astype(BF16)
    row = lax.broadcasted_iota(jnp.int32, (BLOCK, BLOCK), 0)
    col = lax.broadcasted_iota(jnp.int32, (BLOCK, BLOCK), 1)
    causal = col <= row
    for g in range(GMLP_GROUPS):
        w = jnp.where(causal, ws_ref[g], 0.0).astype(BF16)
        bias = bs_ref[:, g:g + 1]
        cs = slice(g * GMLP_GROUP_CH, (g + 1) * GMLP_GROUP_CH)
        for c in range(x.shape[0] // BLOCK):
            rs = slice(c * BLOCK, (c + 1) * BLOCK)
            sv = _dot(w, vn[rs, cs]) + bias
            o_ref[rs, cs] = (u[rs, cs] * sv).astype(o_ref.dtype)


def _gmlp(xb, w_uv, ln_g, ln_b, w_s, b_s_t, *, tm=512):
    m = xb.shape[0]
    return pl.pallas_call(
        _gmlp_kernel,
        grid=(m // tm,),
        in_specs=[
            pl.BlockSpec((tm, D_MODEL), lambda i: (i, 0)),
            pl.BlockSpec((D_MODEL, 2 * GMLP_WIDTH), lambda i: (0, 0)),
            pl.BlockSpec((1, GMLP_WIDTH), lambda i: (0, 0)),
            pl.BlockSpec((1, GMLP_WIDTH), lambda i: (0, 0)),
            pl.BlockSpec((GMLP_GROUPS, BLOCK, BLOCK), lambda i: (0, 0, 0)),
            pl.BlockSpec((BLOCK, GMLP_GROUPS), lambda i: (0, 0)),
        ],
        out_specs=pl.BlockSpec((tm, GMLP_WIDTH), lambda i: (i, 0)),
        out_shape=jax.ShapeDtypeStruct((m, GMLP_WIDTH), BF16),
        compiler_params=_params("parallel"),
        name="gmlp_branch",
    )(xb, w_uv, ln_g, ln_b, w_s, b_s_t)


ROPE_Q_TILE = 3
ROPE_K_TILE = 4


def _attn_proj_kernel(x_ref, w_ref, cos_ref, sin_ref, o_ref):
    j = pl.program_id(1)
    acc = _dot(x_ref[...], w_ref[...])
    scale = jnp.where((j == 0) | (j == ROPE_Q_TILE), QK_SCALE, 1.0).astype(F32)
    is_rope = (j == ROPE_Q_TILE) | (j == ROPE_K_TILE)

    @pl.when(is_rope)
    def _():
        cos = cos_ref[...] * scale
        sin = sin_ref[...] * scale
        for h in range(acc.shape[1] // HEAD_DIM):
            cs = slice(h * HEAD_DIM, (h + 1) * HEAD_DIM)
            a = acc[:, cs]
            o_ref[:, cs] = (a * cos + pltpu.roll(a, HEAD_DIM // 2, 1) * sin).astype(o_ref.dtype)

    @pl.when(jnp.logical_not(is_rope))
    def _():
        o_ref[...] = (acc * scale).astype(o_ref.dtype)


def _attn_proj(xb, w_att, cos_t, sin_t, *, tm=512, tn=1024):
    m = xb.shape[0]
    s_blocks = cos_t.shape[0] // tm
    return pl.pallas_call(
        _attn_proj_kernel,
        grid=(m // tm, ATT_WIDTH // tn),
        in_specs=[
            pl.BlockSpec((tm, D_MODEL), lambda i, j: (i, 0)),
            pl.BlockSpec((D_MODEL, tn), lambda i, j: (0, j)),
            pl.BlockSpec((tm, HEAD_DIM), lambda i, j: (i % s_blocks, 0)),
            pl.BlockSpec((tm, HEAD_DIM), lambda i, j: (i % s_blocks, 0)),
        ],
        out_specs=pl.BlockSpec((tm, tn), lambda i, j: (i, j)),
        out_shape=jax.ShapeDtypeStruct((m, ATT_WIDTH), BF16),
        compiler_params=_params("parallel", "arbitrary"),
        name="attn_proj",
    )(xb, w_att, cos_t, sin_t)


def _forget_kernel(x_ref, w_ref, b_ref, c_ref, carry_ref):
    @pl.when(pl.program_id(1) == 0)
    def _():
        carry_ref[...] = jnp.zeros_like(carry_ref)

    z = _dot(x_ref[...], w_ref[...]) + b_ref[...]
    log_f = jnp.minimum(z, 0.0) - jnp.log1p(jnp.exp(-jnp.abs(z)))
    log_f_t = log_f.T[:FOX_HEADS]
    row = lax.broadcasted_iota(jnp.int32, (LANES, LANES), 0)
    col = lax.broadcasted_iota(jnp.int32, (LANES, LANES), 1)
    upper = (row <= col).astype(F32)
    carry = carry_ref[...]
    for k in range(log_f_t.shape[1] // LANES):
        ls = slice(k * LANES, (k + 1) * LANES)
        c = jnp.dot(log_f_t[:, ls], upper, preferred_element_type=F32,
                    precision=lax.Precision.HIGHEST) + carry
        c_ref[0, :, ls] = c
        carry = c[:, LANES - 1:]
    carry_ref[...] = carry


def _forget_cumsum(xb, w_ff, b_ff, batch, *, tc=512):
    m = xb.shape[0]
    seq = m // batch
    nblk = seq // tc
    return pl.pallas_call(
        _forget_kernel,
        grid=(batch, nblk),
        in_specs=[
            pl.BlockSpec((tc, D_MODEL), lambda b, s: (b * nblk + s, 0)),
            pl.BlockSpec((D_MODEL, LANES), lambda b, s: (0, 0)),
            pl.BlockSpec((1, LANES), lambda b, s: (0, 0)),
        ],
        out_specs=pl.BlockSpec((1, FOX_HEADS, tc), lambda b, s: (b, 0, s)),
        out_shape=jax.ShapeDtypeStruct((batch, FOX_HEADS, seq), F32),
        scratch_shapes=[pltpu.VMEM((FOX_HEADS, 1), F32)],
        compiler_params=_params("parallel", "arbitrary"),
        name="forget_cumsum",
    )(xb, w_ff, b_ff)


def _online_softmax_step(q, k, v, bias_row, mask, state):
    m, l, acc = state
    s = _dot_nt(q, k)
    if bias_row is not None:
        s = s - bias_row
    if mask is not None:
        s = jnp.where(mask, s, NEG_INF)
    m_new = jnp.maximum(m, jnp.max(s, axis=-1, keepdims=True))
    p = jnp.exp(s - m_new)
    alpha = jnp.exp(m - m_new)
    l = alpha * l + jnp.sum(p, axis=-1, keepdims=True)
    acc = alpha * acc + _dot(p.astype(BF16), v)
    return m_new, l, acc


def _init_state(tq, dv):
    return (jnp.full((tq, 1), NEG_INF, F32), jnp.zeros((tq, 1), F32), jnp.zeros((tq, dv), F32))


def _diag_mask(t):
    row = lax.broadcasted_iota(jnp.int32, (t, t), 0)
    col = lax.broadcasted_iota(jnp.int32, (t, t), 1)
    return col <= row


def _fox_kernel(q_ref, k_ref, v_ref, c_ref, o_ref, *, t):
    qi = pl.program_id(2)
    q = q_ref[...]

    def step(kb, mask, state):
        start = pl.multiple_of(kb * t, t)
        return _online_softmax_step(q, k_ref[pl.ds(start, t), :], v_ref[pl.ds(start, t), :],
                                    c_ref[0, :, pl.ds(start, t)], mask, state)

    state = lax.fori_loop(0, qi, lambda kb, st: step(kb, None, st), _init_state(t, HEAD_DIM))
    _, l, acc = step(qi, _diag_mask(t), state)
    o_ref[...] = (acc / l).astype(o_ref.dtype)


def _fox_attention(qkv, c, batch, *, t=512):
    m = qkv.shape[0]
    seq = m // batch
    nq = seq // t
    k_col = FOX_WIDTH // HEAD_DIM
    v_col = 2 * FOX_WIDTH // HEAD_DIM
    c3 = c.reshape(batch * FOX_HEADS, 1, seq)
    return pl.pallas_call(
        functools.partial(_fox_kernel, t=t),
        grid=(batch, FOX_HEADS, nq),
        in_specs=[
            pl.BlockSpec((t, HEAD_DIM), lambda b, h, i: (b * nq + i, h)),
            pl.BlockSpec((seq, HEAD_DIM), lambda b, h, i: (b, k_col + h)),
            pl.BlockSpec((seq, HEAD_DIM), lambda b, h, i: (b, v_col + h)),
            pl.BlockSpec((1, 1, seq), lambda b, h, i: (b * FOX_HEADS + h, 0, 0)),
        ],
        out_specs=pl.BlockSpec((t, HEAD_DIM), lambda b, h, i: (b * nq + i, h)),
        out_shape=jax.ShapeDtypeStruct((m, FOX_WIDTH), BF16),
        compiler_params=_params("parallel", "parallel", "arbitrary"),
        name="fox_attention",
    )(qkv, qkv, qkv, c3)


def _diff_kernel(q1_ref, q2_ref, k1_ref, k2_ref, v_ref, lq1_ref, lk1_ref, lq2_ref, lk2_ref, g_ref,
                 o_ref, *, t, lam_init):
    qi = pl.program_id(2)
    q1 = q1_ref[...]
    q2 = q2_ref[...]

    def step(kb, mask, states):
        start = pl.multiple_of(kb * t, t)
        v = v_ref[pl.ds(start, t), :]
        s1 = _online_softmax_step(q1, k1_ref[pl.ds(start, t), :], v, None, mask, states[0])
        s2 = _online_softmax_step(q2, k2_ref[pl.ds(start, t), :], v, None, mask, states[1])
        return s1, s2

    init = (_init_state(t, DIFF_V_DIM), _init_state(t, DIFF_V_DIM))
    states = lax.fori_loop(0, qi, lambda kb, st: step(kb, None, st), init)
    (_, l1, acc1), (_, l2, acc2) = step(qi, _diag_mask(t), states)

    lam = (jnp.exp(jnp.sum(lq1_ref[...] * lk1_ref[...])) - jnp.exp(jnp.sum(lq2_ref[...] * lk2_ref[...]))
           + lam_init)
    o = acc1 / l1 - lam * (acc2 / l2)
    o = o * lax.rsqrt(jnp.mean(o * o, axis=-1, keepdims=True) + RMS_EPS) * g_ref[...]
    o_ref[...] = (o * (1.0 - lam_init)).astype(o_ref.dtype)


def _diff_attention(qkv, lq1, lk1, lq2, lk2, norm_g, batch, lam_init, *, t=512):
    m = qkv.shape[0]
    seq = m // batch
    nq = seq // t
    q_col = 3 * FOX_WIDTH // HEAD_DIM
    k_col = q_col + DIFF_QK_WIDTH // HEAD_DIM
    v_col = (3 * FOX_WIDTH + 2 * DIFF_QK_WIDTH) // DIFF_V_DIM
    vec = pl.BlockSpec((1, HEAD_DIM), lambda b, h, i: (0, 0))
    return pl.pallas_call(
        functools.partial(_diff_kernel, t=t, lam_init=lam_init),
        grid=(batch, DIFF_HEADS, nq),
        in_specs=[
            pl.BlockSpec((t, HEAD_DIM), lambda b, h, i: (b * nq + i, q_col + 2 * h)),
            pl.BlockSpec((t, HEAD_DIM), lambda b, h, i: (b * nq + i, q_col + 2 * h + 1)),
            pl.BlockSpec((seq, HEAD_DIM), lambda b, h, i: (b, k_col + 2 * h)),
            pl.BlockSpec((seq, HEAD_DIM), lambda b, h, i: (b, k_col + 2 * h + 1)),
            pl.BlockSpec((seq, DIFF_V_DIM), lambda b, h, i: (b, v_col + h)),
            vec, vec, vec, vec,
            pl.BlockSpec((1, DIFF_V_DIM), lambda b, h, i: (0, 0)),
        ],
        out_specs=pl.BlockSpec((t, DIFF_V_DIM), lambda b, h, i: (b * nq + i, h)),
        out_shape=jax.ShapeDtypeStruct((m, DIFF_WIDTH), BF16),
        compiler_params=_params("parallel", "parallel", "arbitrary"),
        name="diff_attention",
    )(qkv, qkv, qkv, qkv, qkv, lq1, lk1, lq2, lk2, norm_g)


def _merge_kernel(x_ref, a_ref, b_ref, c_ref, wg0_ref, wg1_ref, wg2_ref, bg0_ref, bg1_ref, bg2_ref,
                  wa_ref, wb_ref, wc_ref, o_ref):
    x = x_ref[...]
    out = None
    for br_ref, wg_ref, bg_ref, w_ref in ((a_ref, wg0_ref, bg0_ref, wa_ref),
                                          (b_ref, wg1_ref, bg1_ref, wb_ref),
                                          (c_ref, wg2_ref, bg2_ref, wc_ref)):
        gate = jax.nn.sigmoid(_dot(x, wg_ref[...]) + bg_ref[...])
        term = gate * _dot(br_ref[...], w_ref[...])
        out = term if out is None else out + term
    o_ref[...] = out.astype(o_ref.dtype)


def _merge(xb, out_a, out_b, out_c, w_gate, b_gate, w_a, w_b, w_c, *, tm=512, tn=512):
    m = xb.shape[0]
    nj = D_MODEL // tn
    act = lambda w: pl.BlockSpec((tm, w), lambda i, j: (i, 0))
    gate_w = lambda k: pl.BlockSpec((D_MODEL, tn), lambda i, j: (0, k * nj + j))
    gate_b = lambda k: pl.BlockSpec((1, tn), lambda i, j: (0, k * nj + j))
    branch_w = lambda w: pl.BlockSpec((w, tn), lambda i, j: (0, j))
    return pl.pallas_call(
        _merge_kernel,
        grid=(m // tm, nj),
        in_specs=[act(D_MODEL), act(GMLP_WIDTH), act(FOX_WIDTH), act(DIFF_WIDTH),
                  gate_w(0), gate_w(1), gate_w(2), gate_b(0), gate_b(1), gate_b(2),
                  branch_w(GMLP_WIDTH), branch_w(FOX_WIDTH), branch_w(DIFF_WIDTH)],
        out_specs=pl.BlockSpec((tm, tn), lambda i, j: (i, j)),
        out_shape=jax.ShapeDtypeStruct((m, D_MODEL), BF16),
        compiler_params=_params("parallel", "arbitrary"),
        name="gated_merge",
    )(xb, out_a, out_b, out_c, w_gate, w_gate, w_gate, b_gate, b_gate, b_gate, w_a, w_b, w_c)


def _out_ln_kernel(m_ref, w_ref, x_ref, g_ref, b_ref, o_ref, ob_ref):
    y = DEEPNORM_ALPHA * x_ref[...] + _dot(m_ref[...], w_ref[...])
    y = _layer_norm(y, g_ref[...], b_ref[...])
    o_ref[...] = y
    ob_ref[...] = y.astype(ob_ref.dtype)


def _out_ln(merged, w_out, x, ln_g, ln_b, *, tm=512):
    m = x.shape[0]
    row = pl.BlockSpec((tm, D_MODEL), lambda i: (i, 0))
    vec = pl.BlockSpec((1, D_MODEL), lambda i: (0, 0))
    return pl.pallas_call(
        _out_ln_kernel,
        grid=(m // tm,),
        in_specs=[row, pl.BlockSpec((D_MODEL, D_MODEL), lambda i: (0, 0)), row, vec, vec],
        out_specs=[row, row],
        out_shape=[jax.ShapeDtypeStruct((m, D_MODEL), F32), jax.ShapeDtypeStruct((m, D_MODEL), BF16)],
        compiler_params=_params("parallel"),
        name="out_proj_ln",
    )(merged, w_out, x, ln_g, ln_b)


def _mlp_kernel(xb_ref, wu_ref, wd_ref, x_ref, g_ref, b_ref, o_ref, ob_ref, acc_ref):
    f = pl.program_id(1)

    @pl.when(f == 0)
    def _():
        acc_ref[...] = jnp.zeros_like(acc_ref)

    h = jnp.maximum(_dot(xb_ref[...], wu_ref[...]), 0.0)
    acc_ref[...] += _dot((h * h).astype(BF16), wd_ref[...])

    @pl.when(f == pl.num_programs(1) - 1)
    def _():
        y = _layer_norm(DEEPNORM_ALPHA * x_ref[...] + acc_ref[...], g_ref[...], b_ref[...])
        o_ref[...] = y
        ob_ref[...] = y.astype(ob_ref.dtype)


def _mlp(xb, w_up, w_down, x, ln_g, ln_b, *, tm=512, tf=512):
    m = x.shape[0]
    row = pl.BlockSpec((tm, D_MODEL), lambda i, f: (i, 0))
    vec = pl.BlockSpec((1, D_MODEL), lambda i, f: (0, 0))
    return pl.pallas_call(
        _mlp_kernel,
        grid=(m // tm, D_FF // tf),
        in_specs=[row,
                  pl.BlockSpec((D_MODEL, tf), lambda i, f: (0, f)),
                  pl.BlockSpec((tf, D_MODEL), lambda i, f: (f, 0)),
                  row, vec, vec],
        out_specs=[row, row],
        out_shape=[jax.ShapeDtypeStruct((m, D_MODEL), F32), jax.ShapeDtypeStruct((m, D_MODEL), BF16)],
        scratch_shapes=[pltpu.VMEM((tm, D_MODEL), F32)],
        compiler_params=_params("parallel", "arbitrary"),
        name="relu2_mlp_ln",
    )(xb, w_up, w_down, x, ln_g, ln_b)


def kernel(x, w_in, b_forget, gmlp_ln_g, gmlp_ln_b, gmlp_w_s, gmlp_b_s, lam_q1, lam_k1, lam_q2, lam_k2,
           diff_norm_g, w_branch_a, w_branch_b, w_branch_c, w_gate, b_gate, w_out, ln_mix_g, ln_mix_b,
           w_up, w_down, ln_mlp_g, ln_mlp_b):
    batch, seq, d = x.shape
    m = batch * seq

    pos = jnp.arange(seq, dtype=F32)
    inv_freq = ROPE_THETA ** (-jnp.arange(0, HEAD_DIM, 2, dtype=F32) / HEAD_DIM)
    ang = pos[:, None] * inv_freq[None, :]
    cos_t = jnp.concatenate([jnp.cos(ang), jnp.cos(ang)], axis=-1)
    sin_t = jnp.concatenate([-jnp.sin(ang), jnp.sin(ang)], axis=-1)

    w_uv = w_in[:, :, :OFF_FQ].astype(BF16)
    w_att = jnp.concatenate([w_in[:, :, OFF_FQ:OFF_FF], w_in[:, :, OFF_DQ:]], axis=-1).astype(BF16)
    w_ff = jnp.pad(w_in[:, :, OFF_FF:OFF_DQ], ((0, 0), (0, 0), (0, LANES - FOX_HEADS))).astype(BF16)
    b_ff = jnp.pad(b_forget, ((0, 0), (0, LANES - FOX_HEADS)))
    w_gate_b = w_gate.astype(BF16)
    w_a, w_b, w_c = w_branch_a.astype(BF16), w_branch_b.astype(BF16), w_branch_c.astype(BF16)
    w_out_b, w_up_b, w_down_b = w_out.astype(BF16), w_up.astype(BF16), w_down.astype(BF16)
    b_s_t = jnp.swapaxes(gmlp_b_s, 1, 2)

    xf = x.reshape(m, d)
    xb = xf.astype(BF16)
    row = lambda a, l: a[l][None, :]
    for l in range(DEPTH):
        lam_init = 0.8 - 0.6 * math.exp(-0.3 * l)
        out_a = _gmlp(xb, w_uv[l], row(gmlp_ln_g, l), row(gmlp_ln_b, l), gmlp_w_s[l], b_s_t[l])
        qkv = _attn_proj(xb, w_att[l], cos_t, sin_t)
        c = _forget_cumsum(xb, w_ff[l], row(b_ff, l), batch)
        out_b = _fox_attention(qkv, c, batch)
        out_c = _diff_attention(qkv, row(lam_q1, l), row(lam_k1, l), row(lam_q2, l), row(lam_k2, l),
                                row(diff_norm_g, l), batch, lam_init)
        merged = _merge(xb, out_a, out_b, out_c, w_gate_b[l], row(b_gate, l), w_a[l], w_b[l], w_c[l])
        xf, xb = _out_ln(merged, w_out_b[l], xf, row(ln_mix_g, l), row(ln_mix_b, l))
        xf, xb = _mlp(xb, w_up_b[l], w_down_b[l], xf, row(ln_mlp_g, l), row(ln_mlp_b, l))
    return xf.reshape(batch, seq, d)
```

```python
import functools
import math

import jax
import jax.numpy as jnp
from jax import lax
from jax.experimental import pallas as pl
from jax.experimental.pallas import tpu as pltpu

D_MODEL = 2048
DEPTH = 4
HEAD_DIM = 128
BLOCK = 128
GMLP_WIDTH = D_MODEL // 2
GMLP_GROUPS = 4
GMLP_GROUP_CH = GMLP_WIDTH // GMLP_GROUPS
FOX_HEADS = D_MODEL // 256
FOX_WIDTH = FOX_HEADS * HEAD_DIM
DIFF_HEADS = D_MODEL // 512
DIFF_QK_WIDTH = DIFF_HEADS * 2 * HEAD_DIM
DIFF_V_DIM = 2 * HEAD_DIM
DIFF_WIDTH = DIFF_HEADS * DIFF_V_DIM
N_BRANCH = 3
D_FF = 4 * D_MODEL
ROPE_THETA = 10000.0
LN_EPS = 1e-5
RMS_EPS = 1e-5
NEG_INF = -1e30
DEEPNORM_ALPHA = (2 * DEPTH) ** 0.25

OFF_V = GMLP_WIDTH
OFF_FQ = OFF_V + GMLP_WIDTH
OFF_FK = OFF_FQ + FOX_WIDTH
OFF_FV = OFF_FK + FOX_WIDTH
OFF_FF = OFF_FV + FOX_WIDTH
OFF_DQ = OFF_FF + FOX_HEADS
OFF_DK = OFF_DQ + DIFF_QK_WIDTH
OFF_DV = OFF_DK + DIFF_QK_WIDTH

LOG2E = math.log2(math.e)
Q_SCALE = HEAD_DIM ** -0.5 * LOG2E
LANES = 128
VMEM_LIMIT = 60 * 1024 * 1024

T_FQ, T_FV, T_DQ, T_DV = range(4)
SECTION = 1024

BF16 = jnp.bfloat16
F32 = jnp.float32
SINGLE = pl.Buffered(1)


def _params(*sem):
    return pltpu.CompilerParams(dimension_semantics=sem, vmem_limit_bytes=VMEM_LIMIT)


def _dot(a, b):
    return jnp.dot(a, b, preferred_element_type=F32)


def _dot_nt(a, b):
    return lax.dot_general(a, b, (((1,), (1,)), ((), ())), preferred_element_type=F32)


def _gelu(x):
    return 0.5 * x * (1.0 + lax.erf(x * (2.0 ** -0.5)))


def _layer_norm(y, g, b):
    mu = jnp.mean(y, axis=-1, keepdims=True)
    d = y - mu
    var = jnp.mean(d * d, axis=-1, keepdims=True)
    return d * lax.rsqrt(var + LN_EPS) * g + b


def _gmlp_kernel(x_ref, w_ref, g_ref, b_ref, ws_ref, bs_ref, o_ref, wb_ref):
    @pl.when(pl.program_id(0) == 0)
    def _():
        wb_ref[...] = w_ref[...].astype(BF16)

    x = x_ref[...]
    u = _gelu(_dot(x, wb_ref[:, :GMLP_WIDTH]))
    v = _gelu(_dot(x, wb_ref[:, GMLP_WIDTH:]))
    vn = _layer_norm(v, g_ref[...], b_ref[...]).astype(BF16)
    row = lax.broadcasted_iota(jnp.int32, (BLOCK, BLOCK), 0)
    col = lax.broadcasted_iota(jnp.int32, (BLOCK, BLOCK), 1)
    causal = col <= row
    for g in range(GMLP_GROUPS):
        w = jnp.where(causal, ws_ref[g], 0.0).astype(BF16)
        bias = bs_ref[:, g:g + 1]
        cs = slice(g * GMLP_GROUP_CH, (g + 1) * GMLP_GROUP_CH)
        for c in range(x.shape[0] // BLOCK):
            rs = slice(c * BLOCK, (c + 1) * BLOCK)
            sv = _dot(w, vn[rs, cs]) + bias
            o_ref[rs, cs] = (u[rs, cs] * sv).astype(o_ref.dtype)


def _gmlp(xb, w_in, ln_g, ln_b, w_s, b_s_t, layer, *, tm=512):
    m = xb.shape[0]
    uv = 2 * GMLP_WIDTH
    return pl.pallas_call(
        _gmlp_kernel,
        grid=(m // tm,),
        in_specs=[
            pl.BlockSpec((tm, D_MODEL), lambda i: (i, 0)),
            pl.BlockSpec((None, D_MODEL, uv), lambda i: (layer, 0, 0), pipeline_mode=SINGLE),
            pl.BlockSpec((1, GMLP_WIDTH), lambda i: (0, 0)),
            pl.BlockSpec((1, GMLP_WIDTH), lambda i: (0, 0)),
            pl.BlockSpec((GMLP_GROUPS, BLOCK, BLOCK), lambda i: (0, 0, 0)),
            pl.BlockSpec((BLOCK, GMLP_GROUPS), lambda i: (0, 0)),
        ],
        out_specs=pl.BlockSpec((tm, GMLP_WIDTH), lambda i: (i, 0)),
        out_shape=jax.ShapeDtypeStruct((m, GMLP_WIDTH), BF16),
        scratch_shapes=[pltpu.VMEM((D_MODEL, uv), BF16)],
        compiler_params=_params("arbitrary"),
        name="gmlp_branch",
    )(xb, w_in, ln_g, ln_b, w_s, b_s_t)


def _proj_keys_kernel(x_ref, w_ref, cos_ref, sin_ref, o_ref):
    acc = _dot(x_ref[...], w_ref[...])

    @pl.when(pl.program_id(1) == 0)
    def _():
        o_ref[...] = acc.astype(o_ref.dtype)

    @pl.when(pl.program_id(1) == 1)
    def _():
        cos = cos_ref[...]
        sin = sin_ref[...]
        for h in range(acc.shape[1] // HEAD_DIM):
            cs = slice(h * HEAD_DIM, (h + 1) * HEAD_DIM)
            a = acc[:, cs]
            o_ref[:, cs] = (a * cos + pltpu.roll(a, HEAD_DIM // 2, 1) * sin).astype(o_ref.dtype)


def _proj_keys(xb, w_keys, cos_t, sin_t, *, tm=512):
    m = xb.shape[0]
    s_blocks = cos_t.shape[0] // tm
    return pl.pallas_call(
        _proj_keys_kernel,
        grid=(m // tm, 2),
        in_specs=[
            pl.BlockSpec((tm, D_MODEL), lambda i, j: (i, 0)),
            pl.BlockSpec((D_MODEL, SECTION), lambda i, j: (0, j)),
            pl.BlockSpec((tm, HEAD_DIM), lambda i, j: (i % s_blocks, 0)),
            pl.BlockSpec((tm, HEAD_DIM), lambda i, j: (i % s_blocks, 0)),
        ],
        out_specs=pl.BlockSpec((tm, SECTION), lambda i, j: (i, j)),
        out_shape=jax.ShapeDtypeStruct((m, 2 * SECTION), BF16),
        compiler_params=_params("parallel", "arbitrary"),
        name="proj_keys",
    )(xb, w_keys, cos_t, sin_t)


def _proj_t_kernel(w_ref, x_ref, cos_ref, sin_ref, o_ref):
    j = pl.program_id(1)
    acc = _dot_nt(w_ref[...], x_ref[...])
    half = HEAD_DIM // 2

    @pl.when(j == T_FQ)
    def _():
        o_ref[...] = (acc * Q_SCALE).astype(o_ref.dtype)

    @pl.when((j == T_FV) | (j == T_DV))
    def _():
        o_ref[...] = acc.astype(o_ref.dtype)

    @pl.when(j == T_DQ)
    def _():
        cos = cos_ref[...] * Q_SCALE
        sin = sin_ref[...] * Q_SCALE
        for h in range(SECTION // HEAD_DIM):
            a = acc[h * HEAD_DIM:h * HEAD_DIM + half]
            b = acc[h * HEAD_DIM + half:(h + 1) * HEAD_DIM]
            o_ref[h * HEAD_DIM:h * HEAD_DIM + half] = (a * cos - b * sin).astype(o_ref.dtype)
            o_ref[h * HEAD_DIM + half:(h + 1) * HEAD_DIM] = (b * cos + a * sin).astype(o_ref.dtype)


def _proj_t(xb, w_t, cos_tt, sin_tt, batch, *, tm=512):
    m = xb.shape[0]
    seq = m // batch
    s_blocks = seq // tm
    n_sec = w_t.shape[0] // SECTION
    return pl.pallas_call(
        _proj_t_kernel,
        grid=(m // tm, n_sec),
        in_specs=[
            pl.BlockSpec((SECTION, D_MODEL), lambda i, j: (j, 0)),
            pl.BlockSpec((tm, D_MODEL), lambda i, j: (i, 0)),
            pl.BlockSpec((HEAD_DIM // 2, tm), lambda i, j: (0, i % s_blocks)),
            pl.BlockSpec((HEAD_DIM // 2, tm), lambda i, j: (0, i % s_blocks)),
        ],
        out_specs=pl.BlockSpec((None, SECTION, tm), lambda i, j: (i // s_blocks, j, i % s_blocks)),
        out_shape=jax.ShapeDtypeStruct((batch, n_sec * SECTION, seq), BF16),
        compiler_params=_params("parallel", "arbitrary"),
        name="proj_transposed",
    )(w_t, xb, cos_tt, sin_tt)


def _forget_kernel(x_ref, w_ref, b_ref, c_ref, carry_ref):
    @pl.when(pl.program_id(1) == 0)
    def _():
        carry_ref[...] = jnp.zeros_like(carry_ref)

    z = _dot(x_ref[...], w_ref[...]) + b_ref[...]
    log_f = jnp.minimum(z, 0.0) - jnp.log1p(jnp.exp(-jnp.abs(z)))
    tc = z.shape[0]
    row = lax.broadcasted_iota(jnp.int32, (tc, tc), 0)
    col = lax.broadcasted_iota(jnp.int32, (tc, tc), 1)
    lower = (col <= row).astype(F32)
    c = jnp.dot(lower, log_f, preferred_element_type=F32, precision=lax.Precision.HIGHEST) + carry_ref[...]
    carry_ref[...] = c[tc - 1:, :]
    for h in range(FOX_HEADS):
        c_ref[h] = jnp.broadcast_to(c[:, h:h + 1], (tc, LANES)) * LOG2E


def _forget_cumsum(xb, w_ff, b_ff, batch, *, tc=512):
    m = xb.shape[0]
    seq = m // batch
    nblk = seq // tc
    return pl.pallas_call(
        _forget_kernel,
        grid=(batch, nblk),
        in_specs=[
            pl.BlockSpec((tc, D_MODEL), lambda b, s: (b * nblk + s, 0)),
            pl.BlockSpec((D_MODEL, LANES), lambda b, s: (0, 0)),
            pl.BlockSpec((1, LANES), lambda b, s: (0, 0)),
        ],
        out_specs=pl.BlockSpec((None, FOX_HEADS, tc, LANES), lambda b, s: (b, 0, s, 0)),
        out_shape=jax.ShapeDtypeStruct((batch, FOX_HEADS, seq, LANES), F32),
        scratch_shapes=[pltpu.VMEM((1, LANES), F32)],
        compiler_params=_params("parallel", "arbitrary"),
        name="forget_cumsum",
    )(xb, w_ff, b_ff)


def _softmax_step_t(s, vt, mask, m, l, acc_ref):
    if mask is not None:
        s = jnp.where(mask, s, NEG_INF)
    m_new = jnp.maximum(m, jnp.max(s, axis=0, keepdims=True))
    p = jnp.exp2(s - m_new)
    alpha = jnp.exp2(m - m_new)
    l = alpha * l + jnp.sum(p, axis=0, keepdims=True)
    acc_ref[...] = acc_ref[...] * alpha + _dot(vt, p.astype(BF16))
    return m_new, l


def _causal_mask_t(kb, qi, t):
    key = lax.broadcasted_iota(jnp.int32, (t, t), 0) + kb * t
    query = lax.broadcasted_iota(jnp.int32, (t, t), 1) + qi * t
    return key <= query


def _init_stats(t):
    return jnp.full((1, t), NEG_INF, F32), jnp.zeros((1, t), F32)


def _fox_kernel(qt_ref, k_ref, vt_ref, c_ref, o_ref, acc_ref, *, t):
    qi = pl.program_id(2)
    acc_ref[...] = jnp.zeros_like(acc_ref)

    def step(kb, masked, stats):
        start = pl.multiple_of(kb * t, t)
        s = _dot(k_ref[pl.ds(start, t), :], qt_ref[...])
        s = s - pltpu.repeat(c_ref[pl.ds(start, t), :], t // LANES, axis=1)
        mask = _causal_mask_t(kb, qi, t) if masked else None
        return _softmax_step_t(s, vt_ref[:, pl.ds(start, t)], mask, *stats, acc_ref)

    stats = lax.fori_loop(0, qi, lambda kb, st: step(kb, False, st), _init_stats(t))
    _, l = step(qi, True, stats)
    o_ref[...] = (acc_ref[...] / l).T.astype(o_ref.dtype)


def _fox_attention(keys, proj_t, c, batch, *, t=1024):
    m = keys.shape[0]
    seq = m // batch
    nq = seq // t
    q_row = T_FQ * SECTION // HEAD_DIM
    v_row = T_FV * SECTION // HEAD_DIM
    return pl.pallas_call(
        functools.partial(_fox_kernel, t=t),
        grid=(batch, FOX_HEADS, nq),
        in_specs=[
            pl.BlockSpec((None, HEAD_DIM, t), lambda b, h, i: (b, q_row + h, i)),
            pl.BlockSpec((seq, HEAD_DIM), lambda b, h, i: (b, h)),
            pl.BlockSpec((None, HEAD_DIM, seq), lambda b, h, i: (b, v_row + h, 0)),
            pl.BlockSpec((None, None, seq, LANES), lambda b, h, i: (b, h, 0, 0)),
        ],
        out_specs=pl.BlockSpec((t, HEAD_DIM), lambda b, h, i: (b * nq + i, h)),
        out_shape=jax.ShapeDtypeStruct((m, FOX_WIDTH), BF16),
        scratch_shapes=[pltpu.VMEM((HEAD_DIM, t), F32)],
        compiler_params=_params("parallel", "parallel", "arbitrary"),
        name="fox_attention",
    )(proj_t, keys, proj_t, c)


def _diff_kernel(q1_ref, q2_ref, k1_ref, k2_ref, vt_ref, lq1_ref, lk1_ref, lq2_ref, lk2_ref, g_ref,
                 o_ref, acc1_ref, acc2_ref, *, t, lam_init):
    qi = pl.program_id(2)
    acc1_ref[...] = jnp.zeros_like(acc1_ref)
    acc2_ref[...] = jnp.zeros_like(acc2_ref)

    def step(kb, masked, stats):
        start = pl.multiple_of(kb * t, t)
        vt = vt_ref[:, pl.ds(start, t)]
        mask = _causal_mask_t(kb, qi, t) if masked else None
        s1 = _dot(k1_ref[pl.ds(start, t), :], q1_ref[...])
        st1 = _softmax_step_t(s1, vt, mask, *stats[0], acc1_ref)
        s2 = _dot(k2_ref[pl.ds(start, t), :], q2_ref[...])
        st2 = _softmax_step_t(s2, vt, mask, *stats[1], acc2_ref)
        return st1, st2

    stats = lax.fori_loop(0, qi, lambda kb, st: step(kb, False, st), (_init_stats(t), _init_stats(t)))
    (_, l1), (_, l2) = step(qi, True, stats)

    lam = (jnp.exp(jnp.sum(lq1_ref[...] * lk1_ref[...])) - jnp.exp(jnp.sum(lq2_ref[...] * lk2_ref[...]))
           + lam_init)
    o = acc1_ref[...] / l1 - lam * (acc2_ref[...] / l2)
    o = o * lax.rsqrt(jnp.mean(o * o, axis=0, keepdims=True) + RMS_EPS)
    o = o * pltpu.repeat(g_ref[...], t // LANES, axis=1) * (1.0 - lam_init)
    o_ref[...] = o.T.astype(o_ref.dtype)


def _diff_attention(keys, proj_t, lq1, lk1, lq2, lk2, norm_g_lanes, batch, lam_init, *, t=1024):
    m = keys.shape[0]
    seq = m // batch
    nq = seq // t
    q_row = T_DQ * SECTION // HEAD_DIM
    k_col = SECTION // HEAD_DIM
    v_row = T_DV * SECTION // DIFF_V_DIM
    vec = pl.BlockSpec((1, HEAD_DIM), lambda b, h, i: (0, 0))
    return pl.pallas_call(
        functools.partial(_diff_kernel, t=t, lam_init=lam_init),
        grid=(batch, DIFF_HEADS, nq),
        in_specs=[
            pl.BlockSpec((None, HEAD_DIM, t), lambda b, h, i: (b, q_row + 2 * h, i)),
            pl.BlockSpec((None, HEAD_DIM, t), lambda b, h, i: (b, q_row + 2 * h + 1, i)),
            pl.BlockSpec((seq, HEAD_DIM), lambda b, h, i: (b, k_col + 2 * h)),
            pl.BlockSpec((seq, HEAD_DIM), lambda b, h, i: (b, k_col + 2 * h + 1)),
            pl.BlockSpec((None, DIFF_V_DIM, seq), lambda b, h, i: (b, v_row + h, 0)),
            vec, vec, vec, vec,
            pl.BlockSpec((DIFF_V_DIM, LANES), lambda b, h, i: (0, 0)),
        ],
        out_specs=pl.BlockSpec((t, DIFF_V_DIM), lambda b, h, i: (b * nq + i, h)),
        out_shape=jax.ShapeDtypeStruct((m, DIFF_WIDTH), BF16),
        scratch_shapes=[pltpu.VMEM((DIFF_V_DIM, t), F32), pltpu.VMEM((DIFF_V_DIM, t), F32)],
        compiler_params=_params("parallel", "parallel", "arbitrary"),
        name="diff_attention",
    )(proj_t, proj_t, keys, keys, proj_t, lq1, lk1, lq2, lk2, norm_g_lanes)


def _merge_kernel(x_ref, a_ref, b_ref, c_ref, wg0_ref, wg1_ref, wg2_ref, bg0_ref, bg1_ref, bg2_ref,
                  wa_ref, wb_ref, wc_ref, o_ref):
    x = x_ref[...]
    out = None
    for br_ref, wg_ref, bg_ref, w_ref in ((a_ref, wg0_ref, bg0_ref, wa_ref),
                                          (b_ref, wg1_ref, bg1_ref, wb_ref),
                                          (c_ref, wg2_ref, bg2_ref, wc_ref)):
        gate = jax.nn.sigmoid(_dot(x, wg_ref[...].astype(BF16)) + bg_ref[...])
        term = gate * _dot(br_ref[...], w_ref[...].astype(BF16))
        out = term if out is None else out + term
    o_ref[...] = out.astype(o_ref.dtype)


def _merge(xb, out_a, out_b, out_c, w_gate, b_gate, w_a, w_b, w_c, layer, *, tm=1024, tn=256):
    m = xb.shape[0]
    nj = D_MODEL // tn
    act = lambda w: pl.BlockSpec((tm, w), lambda i, j: (i, 0), pipeline_mode=SINGLE)
    gate_w = lambda k: pl.BlockSpec((None, D_MODEL, tn), lambda i, j: (layer, 0, k * nj + j))
    gate_b = lambda k: pl.BlockSpec((1, tn), lambda i, j: (0, k * nj + j))
    branch_w = lambda w: pl.BlockSpec((None, w, tn), lambda i, j: (layer, 0, j))
    return pl.pallas_call(
        _merge_kernel,
        grid=(m // tm, nj),
        in_specs=[act(D_MODEL), act(GMLP_WIDTH), act(FOX_WIDTH), act(DIFF_WIDTH),
                  gate_w(0), gate_w(1), gate_w(2), gate_b(0), gate_b(1), gate_b(2),
                  branch_w(GMLP_WIDTH), branch_w(FOX_WIDTH), branch_w(DIFF_WIDTH)],
        out_specs=pl.BlockSpec((tm, tn), lambda i, j: (i, j)),
        out_shape=jax.ShapeDtypeStruct((m, D_MODEL), BF16),
        compiler_params=_params("parallel", "arbitrary"),
        name="gated_merge",
    )(xb, out_a, out_b, out_c, w_gate, w_gate, w_gate, b_gate, b_gate, b_gate, w_a, w_b, w_c)


def _out_ln_kernel(m_ref, w_ref, x_ref, g_ref, b_ref, o_ref, ob_ref, wb_ref):
    @pl.when(pl.program_id(0) == 0)
    def _():
        wb_ref[...] = w_ref[...].astype(BF16)

    y = DEEPNORM_ALPHA * x_ref[...] + _dot(m_ref[...], wb_ref[...])
    y = _layer_norm(y, g_ref[...], b_ref[...])
    o_ref[...] = y
    ob_ref[...] = y.astype(ob_ref.dtype)


def _out_ln(merged, w_out, x, ln_g, ln_b, layer, *, tm=512):
    m = x.shape[0]
    row = pl.BlockSpec((tm, D_MODEL), lambda i: (i, 0))
    vec = pl.BlockSpec((1, D_MODEL), lambda i: (0, 0))
    return pl.pallas_call(
        _out_ln_kernel,
        grid=(m // tm,),
        in_specs=[row,
                  pl.BlockSpec((None, D_MODEL, D_MODEL), lambda i: (layer, 0, 0), pipeline_mode=SINGLE),
                  row, vec, vec],
        out_specs=[row, row],
        out_shape=[jax.ShapeDtypeStruct((m, D_MODEL), F32), jax.ShapeDtypeStruct((m, D_MODEL), BF16)],
        scratch_shapes=[pltpu.VMEM((D_MODEL, D_MODEL), BF16)],
        compiler_params=_params("arbitrary"),
        name="out_proj_ln",
    )(merged, w_out, x, ln_g, ln_b)


def _mlp_kernel(xb_ref, wu_ref, wd_ref, x_ref, g_ref, b_ref, o_ref, ob_ref):
    f = pl.program_id(1)

    @pl.when(f == 0)
    def _():
        o_ref[...] = jnp.zeros_like(o_ref)

    h = jnp.maximum(_dot(xb_ref[...], wu_ref[...].astype(BF16)), 0.0)
    o_ref[...] += _dot((h * h).astype(BF16), wd_ref[...].astype(BF16))

    @pl.when(f == pl.num_programs(1) - 1)
    def _():
        y = _layer_norm(DEEPNORM_ALPHA * x_ref[...] + o_ref[...], g_ref[...], b_ref[...])
        o_ref[...] = y
        ob_ref[...] = y.astype(ob_ref.dtype)


def _mlp(xb, w_up, w_down, x, ln_g, ln_b, layer, *, tm=1024, tf=512):
    m = x.shape[0]
    row = lambda: pl.BlockSpec((tm, D_MODEL), lambda i, f: (i, 0), pipeline_mode=SINGLE)
    vec = pl.BlockSpec((1, D_MODEL), lambda i, f: (0, 0))
    return pl.pallas_call(
        _mlp_kernel,
        grid=(m // tm, D_FF // tf),
        in_specs=[row(),
                  pl.BlockSpec((None, D_MODEL, tf), lambda i, f: (layer, 0, f)),
                  pl.BlockSpec((None, tf, D_MODEL), lambda i, f: (layer, f, 0)),
                  row(), vec, vec],
        out_specs=[row(), row()],
        out_shape=[jax.ShapeDtypeStruct((m, D_MODEL), F32), jax.ShapeDtypeStruct((m, D_MODEL), BF16)],
        compiler_params=_params("parallel", "arbitrary"),
        name="relu2_mlp_ln",
    )(xb, w_up, w_down, x, ln_g, ln_b)


def kernel(x, w_in, b_forget, gmlp_ln_g, gmlp_ln_b, gmlp_w_s, gmlp_b_s, lam_q1, lam_k1, lam_q2, lam_k2,
           diff_norm_g, w_branch_a, w_branch_b, w_branch_c, w_gate, b_gate, w_out, ln_mix_g, ln_mix_b,
           w_up, w_down, ln_mlp_g, ln_mlp_b):
    batch, seq, d = x.shape
    m = batch * seq

    pos = jnp.arange(seq, dtype=F32)
    inv_freq = ROPE_THETA ** (-jnp.arange(0, HEAD_DIM, 2, dtype=F32) / HEAD_DIM)
    ang = pos[:, None] * inv_freq[None, :]
    cos, sin = jnp.cos(ang), jnp.sin(ang)
    cos_t = jnp.concatenate([cos, cos], axis=-1)
    sin_t = jnp.concatenate([-sin, sin], axis=-1)
    cos_tt, sin_tt = cos.T, sin.T

    w_keys = jnp.concatenate([w_in[:, :, OFF_FK:OFF_FV], w_in[:, :, OFF_DK:OFF_DV]], axis=-1).astype(BF16)
    w_t = jnp.concatenate([w_in[:, :, OFF_FQ:OFF_FK], w_in[:, :, OFF_FV:OFF_FF],
                           w_in[:, :, OFF_DQ:OFF_DK], w_in[:, :, OFF_DV:]], axis=-1)
    w_t = jnp.swapaxes(w_t, 1, 2).astype(BF16)
    w_ff = jnp.pad(w_in[:, :, OFF_FF:OFF_DQ], ((0, 0), (0, 0), (0, LANES - FOX_HEADS))).astype(BF16)
    b_ff = jnp.pad(b_forget, ((0, 0), (0, LANES - FOX_HEADS)))
    b_s_t = jnp.swapaxes(gmlp_b_s, 1, 2)
    norm_g_lanes = jnp.broadcast_to(diff_norm_g[:, :, None], (DEPTH, DIFF_V_DIM, LANES))

    xf = x.reshape(m, d)
    xb = xf.astype(BF16)
    row = lambda a, l: a[l][None, :]
    for l in range(DEPTH):
        lam_init = 0.8 - 0.6 * math.exp(-0.3 * l)
        out_a = _gmlp(xb, w_in, row(gmlp_ln_g, l), row(gmlp_ln_b, l), gmlp_w_s[l], b_s_t[l], l)
        keys = _proj_keys(xb, w_keys[l], cos_t, sin_t)
        proj_t = _proj_t(xb, w_t[l], cos_tt, sin_tt, batch)
        c = _forget_cumsum(xb, w_ff[l], row(b_ff, l), batch)
        out_b = _fox_attention(keys, proj_t, c, batch)
        out_c = _diff_attention(keys, proj_t, row(lam_q1, l), row(lam_k1, l), row(lam_q2, l), row(lam_k2, l),
                                norm_g_lanes[l], batch, lam_init)
        merged = _merge(xb, out_a, out_b, out_c, w_gate, row(b_gate, l), w_branch_a, w_branch_b, w_branch_c, l)
        xf, xb = _out_ln(merged, w_out, xf, row(ln_mix_g, l), row(ln_mix_b, l), l)
        xf, xb = _mlp(xb, w_up, w_down, xf, row(ln_mlp_g, l), row(ln_mlp_b, l), l)
    return xf.reshape(batch, seq, d)
```

```python
import functools
import math

import jax
import jax.numpy as jnp
from jax import lax
from jax.experimental import pallas as pl
from jax.experimental.pallas import tpu as pltpu

D_MODEL = 2048
DEPTH = 4
HEAD_DIM = 128
BLOCK = 128
GMLP_WIDTH = D_MODEL // 2
GMLP_GROUPS = 4
GMLP_GROUP_CH = GMLP_WIDTH // GMLP_GROUPS
FOX_HEADS = D_MODEL // 256
FOX_WIDTH = FOX_HEADS * HEAD_DIM
DIFF_HEADS = D_MODEL // 512
DIFF_QK_WIDTH = DIFF_HEADS * 2 * HEAD_DIM
DIFF_V_DIM = 2 * HEAD_DIM
DIFF_WIDTH = DIFF_HEADS * DIFF_V_DIM
N_BRANCH = 3
D_FF = 4 * D_MODEL
ROPE_THETA = 10000.0
LN_EPS = 1e-5
RMS_EPS = 1e-5
NEG_INF = -1e30
DEEPNORM_ALPHA = (2 * DEPTH) ** 0.25

SECTION = 1024
OFF_FQ = 2 * GMLP_WIDTH
OFF_FF = OFF_FQ + 3 * FOX_WIDTH
OFF_DQ = OFF_FF + FOX_HEADS
assert FOX_WIDTH == DIFF_QK_WIDTH == DIFF_WIDTH == SECTION and OFF_FQ % SECTION == 0

LOG2E = math.log2(math.e)
Q_SCALE = HEAD_DIM ** -0.5 * LOG2E
LANES = 128
VMEM_LIMIT = 60 * 1024 * 1024

BF16 = jnp.bfloat16
F32 = jnp.float32
SINGLE = pl.Buffered(1)


def _params(*sem):
    return pltpu.CompilerParams(dimension_semantics=sem, vmem_limit_bytes=VMEM_LIMIT)


def _dot(a, b):
    return jnp.dot(a, b, preferred_element_type=F32)


def _gelu(x):
    return 0.5 * x * (1.0 + lax.erf(x * (2.0 ** -0.5)))


def _layer_norm(y, g, b):
    mu = jnp.mean(y, axis=-1, keepdims=True)
    d = y - mu
    var = jnp.mean(d * d, axis=-1, keepdims=True)
    return d * lax.rsqrt(var + LN_EPS) * g + b


def _tile_lanes(x, width):
    return jnp.concatenate([x] * (width // x.shape[1]), axis=1)


def _gmlp_kernel(x_ref, w_ref, g_ref, b_ref, ws_ref, bs_ref, o_ref, wb_ref):
    @pl.when(pl.program_id(0) == 0)
    def _():
        wb_ref[...] = w_ref[...].astype(BF16)

    x = x_ref[...]
    u = _gelu(_dot(x, wb_ref[:, :GMLP_WIDTH]))
    v = _gelu(_dot(x, wb_ref[:, GMLP_WIDTH:]))
    vn = _layer_norm(v, g_ref[...], b_ref[...]).astype(BF16)
    row = lax.broadcasted_iota(jnp.int32, (BLOCK, BLOCK), 0)
    col = lax.broadcasted_iota(jnp.int32, (BLOCK, BLOCK), 1)
    causal = col <= row
    for g in range(GMLP_GROUPS):
        w = jnp.where(causal, ws_ref[g], 0.0).astype(BF16)
        bias = bs_ref[:, g:g + 1]
        cs = slice(g * GMLP_GROUP_CH, (g + 1) * GMLP_GROUP_CH)
        for c in range(x.shape[0] // BLOCK):
            rs = slice(c * BLOCK, (c + 1) * BLOCK)
            sv = _dot(w, vn[rs, cs]) + bias
            o_ref[rs, cs] = (u[rs, cs] * sv).astype(o_ref.dtype)


def _gmlp(xb, w_in, ln_g, ln_b, w_s, b_s_t, layer, *, tm=512):
    m = xb.shape[0]
    uv = 2 * GMLP_WIDTH
    return pl.pallas_call(
        _gmlp_kernel,
        grid=(m // tm,),
        in_specs=[
            pl.BlockSpec((tm, D_MODEL), lambda i: (i, 0)),
            pl.BlockSpec((None, D_MODEL, uv), lambda i: (layer, 0, 0), pipeline_mode=SINGLE),
            pl.BlockSpec((1, GMLP_WIDTH), lambda i: (0, 0)),
            pl.BlockSpec((1, GMLP_WIDTH), lambda i: (0, 0)),
            pl.BlockSpec((GMLP_GROUPS, BLOCK, BLOCK), lambda i: (0, 0, 0)),
            pl.BlockSpec((BLOCK, GMLP_GROUPS), lambda i: (0, 0)),
        ],
        out_specs=pl.BlockSpec((tm, GMLP_WIDTH), lambda i: (i, 0)),
        out_shape=jax.ShapeDtypeStruct((m, GMLP_WIDTH), BF16),
        scratch_shapes=[pltpu.VMEM((D_MODEL, uv), BF16)],
        compiler_params=_params("arbitrary"),
        name="gmlp_branch",
    )(xb, w_in, ln_g, ln_b, w_s, b_s_t)


def _proj_kernel(x_ref, w_ref, cos_ref, sin_ref, o_ref, *, kinds, transpose_out):
    j = pl.program_id(0)
    acc = _dot(x_ref[...], w_ref[...].astype(BF16))

    def finish(rotary, scale):
        res = acc
        if rotary:
            cos = cos_ref[...]
            sin = sin_ref[...]
            heads = []
            for h in range(SECTION // HEAD_DIM):
                a = acc[:, h * HEAD_DIM:(h + 1) * HEAD_DIM]
                heads.append(a * cos + pltpu.roll(a, HEAD_DIM // 2, 1) * sin)
            res = jnp.concatenate(heads, axis=1)
        if scale != 1.0:
            res = res * scale
        if transpose_out:
            res = res.T
        o_ref[...] = res.astype(o_ref.dtype)

    for idx, (rotary, scale) in enumerate(kinds):
        pl.when(j == idx)(functools.partial(finish, rotary, scale))


def _proj(xb, w, layer, first_block, block_stride, kinds, cos_t, sin_t, batch, transpose_out, name, *, tm=512):
    m = xb.shape[0]
    seq = m // batch
    s_blocks = seq // tm
    n_sec = len(kinds)
    if transpose_out:
        out_spec = pl.BlockSpec((None, SECTION, tm), lambda j, i: (i // s_blocks, j, i % s_blocks))
        out_shape = jax.ShapeDtypeStruct((batch, n_sec * SECTION, seq), BF16)
    else:
        out_spec = pl.BlockSpec((tm, SECTION), lambda j, i: (i, j))
        out_shape = jax.ShapeDtypeStruct((m, n_sec * SECTION), BF16)
    return pl.pallas_call(
        functools.partial(_proj_kernel, kinds=kinds, transpose_out=transpose_out),
        grid=(n_sec, m // tm),
        in_specs=[
            pl.BlockSpec((tm, D_MODEL), lambda j, i: (i, 0)),
            pl.BlockSpec((None, D_MODEL, SECTION), lambda j, i: (layer, 0, first_block + block_stride * j)),
            pl.BlockSpec((tm, HEAD_DIM), lambda j, i: (i % s_blocks, 0)),
            pl.BlockSpec((tm, HEAD_DIM), lambda j, i: (i % s_blocks, 0)),
        ],
        out_specs=out_spec,
        out_shape=out_shape,
        compiler_params=_params("arbitrary", "parallel"),
        name=name,
    )(xb, w, cos_t, sin_t)


def _forget_kernel(x_ref, w_ref, b_ref, c_ref, carry_ref):
    @pl.when(pl.program_id(1) == 0)
    def _():
        carry_ref[...] = jnp.zeros_like(carry_ref)

    z = _dot(x_ref[...], w_ref[...]) + b_ref[...]
    c = jnp.minimum(z, 0.0) - jnp.log1p(jnp.exp(-jnp.abs(z)))
    tc = z.shape[0]
    row = lax.broadcasted_iota(jnp.int32, (tc, LANES), 0)
    shift = 1
    while shift < tc:
        c = c + jnp.where(row >= shift, pltpu.roll(c, shift, 0), 0.0)
        shift *= 2
    c = c + carry_ref[...]
    carry_ref[...] = c[tc - 1:, :]
    for h in range(FOX_HEADS):
        c_ref[h] = jnp.broadcast_to(c[:, h:h + 1], (tc, LANES)) * LOG2E


def _forget_cumsum(xb, w_ff, b_ff, batch, *, tc=512):
    m = xb.shape[0]
    seq = m // batch
    nblk = seq // tc
    return pl.pallas_call(
        _forget_kernel,
        grid=(batch, nblk),
        in_specs=[
            pl.BlockSpec((tc, D_MODEL), lambda b, s: (b * nblk + s, 0)),
            pl.BlockSpec((D_MODEL, LANES), lambda b, s: (0, 0)),
            pl.BlockSpec((1, LANES), lambda b, s: (0, 0)),
        ],
        out_specs=pl.BlockSpec((None, FOX_HEADS, tc, LANES), lambda b, s: (b, 0, s, 0)),
        out_shape=jax.ShapeDtypeStruct((batch, FOX_HEADS, seq, LANES), F32),
        scratch_shapes=[pltpu.VMEM((1, LANES), F32)],
        compiler_params=_params("parallel", "arbitrary"),
        name="forget_cumsum",
    )(xb, w_ff, b_ff)


def _softmax_step_t(s, vt, mask, m, l, acc_ref):
    if mask is not None:
        s = jnp.where(mask, s, NEG_INF)
    m_new = jnp.maximum(m, jnp.max(s, axis=0, keepdims=True))
    p = jnp.exp2(s - m_new)
    alpha = jnp.exp2(m - m_new)
    l = alpha * l + jnp.sum(p, axis=0, keepdims=True)
    acc_ref[...] = acc_ref[...] * alpha + _dot(vt, p.astype(BF16))
    return m_new, l


def _causal_flash_t(qi, t, score_fns, vt_ref, acc_refs):
    half = t // 2

    def run(start, nk, q_lo, masked, stats):
        vt = vt_ref[:, pl.ds(start, nk)]
        mask = None
        if masked:
            key = lax.broadcasted_iota(jnp.int32, (nk, t - q_lo), 0)
            query = lax.broadcasted_iota(jnp.int32, (nk, t - q_lo), 1)
            mask = key <= query
        out = []
        for fn, acc_ref, (m, l) in zip(score_fns, acc_refs, stats):
            acc = acc_ref if q_lo == 0 else acc_ref.at[:, q_lo:]
            out.append(_softmax_step_t(fn(start, nk, q_lo), vt, mask, m[:, q_lo:], l[:, q_lo:], acc))
        return tuple(out)

    for acc_ref in acc_refs:
        acc_ref[...] = jnp.zeros_like(acc_ref)
    init = tuple((jnp.full((1, t), NEG_INF, F32), jnp.zeros((1, t), F32)) for _ in score_fns)
    stats = lax.fori_loop(0, qi, lambda kb, st: run(pl.multiple_of(kb * t, t), t, 0, False, st), init)
    diag = pl.multiple_of(qi * t, t)
    stats = run(diag, half, 0, True, stats)
    late = run(pl.multiple_of(diag + half, half), half, half, True, stats)
    return [jnp.concatenate([l[:, :half], l_late], axis=1) for (_, l), (_, l_late) in zip(stats, late)]


def _fox_kernel(qt_ref, k_ref, vt_ref, c_ref, o_ref, acc_ref, *, t):
    def scores(start, nk, q_lo):
        s = _dot(k_ref[pl.ds(start, nk), :], qt_ref[:, q_lo:])
        return s - _tile_lanes(c_ref[pl.ds(start, nk), :], t - q_lo)

    (l,) = _causal_flash_t(pl.program_id(2), t, (scores,), vt_ref, (acc_ref,))
    o_ref[...] = (acc_ref[...] / l).T.astype(o_ref.dtype)


def _fox_attention(keys, proj_t, c, batch, *, t=1024):
    m = keys.shape[0]
    seq = m // batch
    nq = seq // t
    return pl.pallas_call(
        functools.partial(_fox_kernel, t=t),
        grid=(batch, FOX_HEADS, nq),
        in_specs=[
            pl.BlockSpec((None, HEAD_DIM, t), lambda b, h, i: (b, h, i)),
            pl.BlockSpec((seq, HEAD_DIM), lambda b, h, i: (b, h)),
            pl.BlockSpec((None, HEAD_DIM, seq), lambda b, h, i: (b, FOX_HEADS + h, 0)),
            pl.BlockSpec((None, None, seq, LANES), lambda b, h, i: (b, h, 0, 0)),
        ],
        out_specs=pl.BlockSpec((t, HEAD_DIM), lambda b, h, i: (b * nq + i, h)),
        out_shape=jax.ShapeDtypeStruct((m, FOX_WIDTH), BF16),
        scratch_shapes=[pltpu.VMEM((HEAD_DIM, t), F32)],
        compiler_params=_params("parallel", "parallel", "arbitrary"),
        name="fox_attention",
    )(proj_t, keys, proj_t, c)


def _diff_kernel(q1_ref, q2_ref, k1_ref, k2_ref, vt_ref, lq1_ref, lk1_ref, lq2_ref, lk2_ref, g_ref,
                 o_ref, acc1_ref, acc2_ref, *, t, lam_init):
    def scores(k_ref, q_ref):
        return lambda start, nk, q_lo: _dot(k_ref[pl.ds(start, nk), :], q_ref[:, q_lo:])

    l1, l2 = _causal_flash_t(pl.program_id(2), t, (scores(k1_ref, q1_ref), scores(k2_ref, q2_ref)),
                             vt_ref, (acc1_ref, acc2_ref))
    lam = (jnp.exp(jnp.sum(lq1_ref[...] * lk1_ref[...])) - jnp.exp(jnp.sum(lq2_ref[...] * lk2_ref[...]))
           + lam_init)
    o = acc1_ref[...] / l1 - lam * (acc2_ref[...] / l2)
    o = o * lax.rsqrt(jnp.mean(o * o, axis=0, keepdims=True) + RMS_EPS)
    o = o * _tile_lanes(g_ref[...], t) * (1.0 - lam_init)
    o_ref[...] = o.T.astype(o_ref.dtype)


def _diff_attention(keys, proj_t, lq1, lk1, lq2, lk2, norm_g_lanes, batch, lam_init, *, t=1024):
    m = keys.shape[0]
    seq = m // batch
    nq = seq // t
    v_row = SECTION // DIFF_V_DIM
    vec = pl.BlockSpec((1, HEAD_DIM), lambda b, h, i: (0, 0))
    return pl.pallas_call(
        functools.partial(_diff_kernel, t=t, lam_init=lam_init),
        grid=(batch, DIFF_HEADS, nq),
        in_specs=[
            pl.BlockSpec((None, HEAD_DIM, t), lambda b, h, i: (b, 2 * h, i)),
            pl.BlockSpec((None, HEAD_DIM, t), lambda b, h, i: (b, 2 * h + 1, i)),
            pl.BlockSpec((seq, HEAD_DIM), lambda b, h, i: (b, 2 * h)),
            pl.BlockSpec((seq, HEAD_DIM), lambda b, h, i: (b, 2 * h + 1)),
            pl.BlockSpec((None, DIFF_V_DIM, seq), lambda b, h, i: (b, v_row + h, 0)),
            vec, vec, vec, vec,
            pl.BlockSpec((DIFF_V_DIM, LANES), lambda b, h, i: (0, 0)),
        ],
        out_specs=pl.BlockSpec((t, DIFF_V_DIM), lambda b, h, i: (b * nq + i, h)),
        out_shape=jax.ShapeDtypeStruct((m, DIFF_WIDTH), BF16),
        scratch_shapes=[pltpu.VMEM((DIFF_V_DIM, t), F32), pltpu.VMEM((DIFF_V_DIM, t), F32)],
        compiler_params=_params("parallel", "parallel", "arbitrary"),
        name="diff_attention",
    )(proj_t, proj_t, keys, keys, proj_t, lq1, lk1, lq2, lk2, norm_g_lanes)


def _merge_kernel(x_ref, a_ref, b_ref, c_ref, wg0_ref, wg1_ref, wg2_ref, bg0_ref, bg1_ref, bg2_ref,
                  wa_ref, wb_ref, wc_ref, o_ref):
    x = x_ref[...]
    out = None
    for br_ref, wg_ref, bg_ref, w_ref in ((a_ref, wg0_ref, bg0_ref, wa_ref),
                                          (b_ref, wg1_ref, bg1_ref, wb_ref),
                                          (c_ref, wg2_ref, bg2_ref, wc_ref)):
        gate = jax.nn.sigmoid(_dot(x, wg_ref[...].astype(BF16)) + bg_ref[...])
        term = gate * _dot(br_ref[...], w_ref[...].astype(BF16))
        out = term if out is None else out + term
    o_ref[...] = out.astype(o_ref.dtype)


def _merge(xb, out_a, out_b, out_c, w_gate, b_gate, w_a, w_b, w_c, layer, *, tm=1024, tn=256):
    m = xb.shape[0]
    nj = D_MODEL // tn
    act = lambda w: pl.BlockSpec((tm, w), lambda i, j: (i, 0), pipeline_mode=SINGLE)
    gate_w = lambda k: pl.BlockSpec((None, D_MODEL, tn), lambda i, j: (layer, 0, k * nj + j))
    gate_b = lambda k: pl.BlockSpec((1, tn), lambda i, j: (0, k * nj + j))
    branch_w = lambda w: pl.BlockSpec((None, w, tn), lambda i, j: (layer, 0, j))
    return pl.pallas_call(
        _merge_kernel,
        grid=(m // tm, nj),
        in_specs=[act(D_MODEL), act(GMLP_WIDTH), act(FOX_WIDTH), act(DIFF_WIDTH),
                  gate_w(0), gate_w(1), gate_w(2), gate_b(0), gate_b(1), gate_b(2),
                  branch_w(GMLP_WIDTH), branch_w(FOX_WIDTH), branch_w(DIFF_WIDTH)],
        out_specs=pl.BlockSpec((tm, tn), lambda i, j: (i, j)),
        out_shape=jax.ShapeDtypeStruct((m, D_MODEL), BF16),
        compiler_params=_params("parallel", "arbitrary"),
        name="gated_merge",
    )(xb, out_a, out_b, out_c, w_gate, w_gate, w_gate, b_gate, b_gate, b_gate, w_a, w_b, w_c)


def _out_ln_kernel(m_ref, w_ref, x_ref, g_ref, b_ref, o_ref, ob_ref, wb_ref):
    @pl.when(pl.program_id(0) == 0)
    def _():
        wb_ref[...] = w_ref[...].astype(BF16)

    y = DEEPNORM_ALPHA * x_ref[...] + _dot(m_ref[...], wb_ref[...])
    y = _layer_norm(y, g_ref[...], b_ref[...])
    o_ref[...] = y
    ob_ref[...] = y.astype(ob_ref.dtype)


def _out_ln(merged, w_out, x, ln_g, ln_b, layer, *, tm=512):
    m = x.shape[0]
    row = pl.BlockSpec((tm, D_MODEL), lambda i: (i, 0))
    vec = pl.BlockSpec((1, D_MODEL), lambda i: (0, 0))
    return pl.pallas_call(
        _out_ln_kernel,
        grid=(m // tm,),
        in_specs=[row,
                  pl.BlockSpec((None, D_MODEL, D_MODEL), lambda i: (layer, 0, 0), pipeline_mode=SINGLE),
                  row, vec, vec],
        out_specs=[row, row],
        out_shape=[jax.ShapeDtypeStruct((m, D_MODEL), F32), jax.ShapeDtypeStruct((m, D_MODEL), BF16)],
        scratch_shapes=[pltpu.VMEM((D_MODEL, D_MODEL), BF16)],
        compiler_params=_params("arbitrary"),
        name="out_proj_ln",
    )(merged, w_out, x, ln_g, ln_b)


def _mlp_kernel(xb_ref, wu_ref, wd_ref, x_ref, g_ref, b_ref, o_ref, ob_ref):
    f = pl.program_id(1)

    @pl.when(f == 0)
    def _():
        o_ref[...] = jnp.zeros_like(o_ref)

    h = jnp.maximum(_dot(xb_ref[...], wu_ref[...].astype(BF16)), 0.0)
    o_ref[...] += _dot((h * h).astype(BF16), wd_ref[...].astype(BF16))

    @pl.when(f == pl.num_programs(1) - 1)
    def _():
        y = _layer_norm(DEEPNORM_ALPHA * x_ref[...] + o_ref[...], g_ref[...], b_ref[...])
        o_ref[...] = y
        ob_ref[...] = y.astype(ob_ref.dtype)


def _mlp(xb, w_up, w_down, x, ln_g, ln_b, layer, *, tm=1024, tf=512):
    m = x.shape[0]
    row = lambda: pl.BlockSpec((tm, D_MODEL), lambda i, f: (i, 0), pipeline_mode=SINGLE)
    vec = pl.BlockSpec((1, D_MODEL), lambda i, f: (0, 0))
    return pl.pallas_call(
        _mlp_kernel,
        grid=(m // tm, D_FF // tf),
        in_specs=[row(),
                  pl.BlockSpec((None, D_MODEL, tf), lambda i, f: (layer, 0, f)),
                  pl.BlockSpec((None, tf, D_MODEL), lambda i, f: (layer, f, 0)),
                  row(), vec, vec],
        out_specs=[row(), row()],
        out_shape=[jax.ShapeDtypeStruct((m, D_MODEL), F32), jax.ShapeDtypeStruct((m, D_MODEL), BF16)],
        compiler_params=_params("parallel", "arbitrary"),
        name="relu2_mlp_ln",
    )(xb, w_up, w_down, x, ln_g, ln_b)


def kernel(x, w_in, b_forget, gmlp_ln_g, gmlp_ln_b, gmlp_w_s, gmlp_b_s, lam_q1, lam_k1, lam_q2, lam_k2,
           diff_norm_g, w_branch_a, w_branch_b, w_branch_c, w_gate, b_gate, w_out, ln_mix_g, ln_mix_b,
           w_up, w_down, ln_mlp_g, ln_mlp_b):
    batch, seq, d = x.shape
    m = batch * seq

    pos = jnp.arange(seq, dtype=F32)
    inv_freq = ROPE_THETA ** (-jnp.arange(0, HEAD_DIM, 2, dtype=F32) / HEAD_DIM)
    ang = pos[:, None] * inv_freq[None, :]
    cos_t = jnp.concatenate([jnp.cos(ang), jnp.cos(ang)], axis=-1)
    sin_t = jnp.concatenate([-jnp.sin(ang), jnp.sin(ang)], axis=-1)

    w_diff = w_in[:, :, OFF_DQ:].astype(BF16)
    w_ff = jnp.pad(w_in[:, :, OFF_FF:OFF_DQ], ((0, 0), (0, 0), (0, LANES - FOX_HEADS))).astype(BF16)
    b_ff = jnp.pad(b_forget, ((0, 0), (0, LANES - FOX_HEADS)))
    b_s_t = jnp.swapaxes(gmlp_b_s, 1, 2)
    norm_g_lanes = jnp.broadcast_to(diff_norm_g[:, :, None], (DEPTH, DIFF_V_DIM, LANES))
    fq_block = OFF_FQ // SECTION

    xf = x.reshape(m, d)
    xb = xf.astype(BF16)
    row = lambda a, l: a[l][None, :]
    plain, query = (False, 1.0), (False, Q_SCALE)
    rope, rope_query = (True, 1.0), (True, Q_SCALE)
    for l in range(DEPTH):
        lam_init = 0.8 - 0.6 * math.exp(-0.3 * l)
        out_a = _gmlp(xb, w_in, row(gmlp_ln_g, l), row(gmlp_ln_b, l), gmlp_w_s[l], b_s_t[l], l)
        proj = functools.partial(_proj, xb, cos_t=cos_t, sin_t=sin_t, batch=batch)
        fox_t = proj(w_in, l, fq_block, 2, (query, plain), transpose_out=True, name="proj_fox_qv")
        fox_k = proj(w_in, l, fq_block + 1, 1, (plain,), transpose_out=False, name="proj_fox_k")
        diff_t = proj(w_diff, l, 0, 2, (rope_query, plain), transpose_out=True, name="proj_diff_qv")
        diff_k = proj(w_diff, l, 1, 1, (rope,), transpose_out=False, name="proj_diff_k")
        c = _forget_cumsum(xb, w_ff[l], row(b_ff, l), batch)
        out_b = _fox_attention(fox_k, fox_t, c, batch)
        out_c = _diff_attention(diff_k, diff_t, row(lam_q1, l), row(lam_k1, l), row(lam_q2, l), row(lam_k2, l),
                                norm_g_lanes[l], batch, lam_init)
        merged = _merge(xb, out_a, out_b, out_c, w_gate, row(b_gate, l), w_branch_a, w_branch_b, w_branch_c, l)
        xf, xb = _out_ln(merged, w_out, xf, row(ln_mix_g, l), row(ln_mix_b, l), l)
        xf, xb = _mlp(xb, w_up, w_down, xf, row(ln_mlp_g, l), row(ln_mlp_b, l), l)
    return xf.reshape(batch, seq, d)
```
